```python
import jax, jax.numpy as jnp
from jax import lax
import numpy as np

D_MODEL = 1024
BATCH = 4
SEQ = 8192
DEPTH = 4
DEC_BATCH = 8
DEC_SEQ = 2048
PAST_LEN = 128

GRID_W = 64
NA_HEADS = 8
NA_HEAD_DIM = 64
NA_WIDTH = NA_HEADS * NA_HEAD_DIM
NA_KH_MAX = 8
NA_KW = 16
MLA_HEADS = 8
MLA_NOPE_DIM = 64
MLA_ROPE_DIM = 32
MLA_V_DIM = 64
MLA_Q_RANK = 256
MLA_KV_RANK = 128
MLA_WIDTH = MLA_HEADS * MLA_V_DIM
ROPE_THETA = 10000.0
Q_BLOCK = 128
N_GROUPS = 4
EXPERTS_PER_GROUP = 8
N_EXPERTS = N_GROUPS * EXPERTS_PER_GROUP
TOP_K = 2
D_EXPERT = 512
MOE_BLOCK = 128
EPS = 1e-6
IN_SPLITS = (NA_WIDTH, 2 * NA_WIDTH, 3 * NA_WIDTH,
             3 * NA_WIDTH + MLA_Q_RANK,
             3 * NA_WIDTH + MLA_Q_RANK + MLA_KV_RANK,
             3 * NA_WIDTH + MLA_Q_RANK + MLA_KV_RANK + MLA_ROPE_DIM)
IN_WIDTH = IN_SPLITS[-1] + 2 * D_MODEL

kernel_name = "hybrid_natten_mla_hmoe_encoder"


def rms_norm(x, g):
    xf = x.astype(jnp.float32)
    var = jnp.mean(xf * xf, axis=-1, keepdims=True)
    return (xf * lax.rsqrt(var + EPS)).astype(x.dtype) * g


def rope_tables(seq_len, dtype):
    inv = 1.0 / (ROPE_THETA ** (jnp.arange(0, MLA_ROPE_DIM, 2, dtype=jnp.float32) / MLA_ROPE_DIM))
    ang = jnp.arange(seq_len, dtype=jnp.float32)[:, None] * inv[None, :]
    return jnp.cos(ang).astype(dtype), jnp.sin(ang).astype(dtype)


def apply_rope(x, cos, sin):
    x1, x2 = jnp.split(x, 2, axis=-1)
    return jnp.concatenate([x1 * cos - x2 * sin, x1 * sin + x2 * cos], axis=-1)


def neighbourhood_attention(q, k, v, rpb):
    b, s, h, dh = q.shape
    rows = s // GRID_W
    kh = min(NA_KH_MAX, rows)
    qg = q.reshape(b, rows, GRID_W, h, dh)
    kg = k.reshape(b, rows, GRID_W, h, dh)
    vg = v.reshape(b, rows, GRID_W, h, dh)
    col = jnp.arange(GRID_W)
    col_start = jnp.clip(col - NA_KW // 2, 0, GRID_W - NA_KW)
    col_idx = col_start[:, None] + jnp.arange(NA_KW)[None, :]
    dc = col_idx - col[:, None]
    scale = dh ** -0.5

    def one_row(r):
        rs = jnp.clip(r - kh // 2, 0, rows - kh)
        q_r = lax.dynamic_index_in_dim(qg, r, axis=1, keepdims=False)
        k_rows = lax.dynamic_slice_in_dim(kg, rs, kh, axis=1)
        v_rows = lax.dynamic_slice_in_dim(vg, rs, kh, axis=1)
        k_nb = k_rows[:, :, col_idx]
        v_nb = v_rows[:, :, col_idx]
        dr = rs + jnp.arange(kh) - r
        bias = rpb[:, dr[:, None, None] + NA_KH_MAX - 1, dc[None] + NA_KW - 1]
        sc = jnp.einsum('bwhd,bxwyhd->bhwxy', q_r, k_nb,
                        preferred_element_type=jnp.float32) * scale
        sc = sc + bias.transpose(0, 2, 1, 3)[None].astype(jnp.float32)
        p = jax.nn.softmax(sc.reshape(b, h, GRID_W, kh * NA_KW), axis=-1)
        p = p.reshape(b, h, GRID_W, kh, NA_KW).astype(v.dtype)
        return jnp.einsum('bhwxy,bxwyhd->bwhd', p, v_nb)

    out = lax.map(one_row, jnp.arange(rows))
    return out.transpose(1, 0, 2, 3, 4).reshape(b, s, h * dh)


def mla_attention(c_q, c_kv, k_rope, q_norm_g, w_uq, kv_norm_g, w_ukv):
    b, s, _ = c_q.shape
    q = (rms_norm(c_q, q_norm_g) @ w_uq).reshape(b, s, MLA_HEADS, MLA_NOPE_DIM + MLA_ROPE_DIM)
    kv = (rms_norm(c_kv, kv_norm_g) @ w_ukv).reshape(b, s, MLA_HEADS, MLA_NOPE_DIM + MLA_V_DIM)
    q_nope, q_rope = jnp.split(q, [MLA_NOPE_DIM], axis=-1)
    k_nope, v = jnp.split(kv, [MLA_NOPE_DIM], axis=-1)
    cos, sin = rope_tables(s, q.dtype)
    q_rope = apply_rope(q_rope, cos[:, None, :], sin[:, None, :])
    k_rope = apply_rope(k_rope, cos, sin)
    scale = (MLA_NOPE_DIM + MLA_ROPE_DIM) ** -0.5
    nq = s // Q_BLOCK
    qn_blocks = q_nope.reshape(b, nq, Q_BLOCK, MLA_HEADS, MLA_NOPE_DIM).transpose(1, 0, 2, 3, 4)
    qr_blocks = q_rope.reshape(b, nq, Q_BLOCK, MLA_HEADS, MLA_ROPE_DIM).transpose(1, 0, 2, 3, 4)

    def one_block(args):
        qn, qr = args
        sc = (jnp.einsum('bqhd,bkhd->bhqk', qn, k_nope, preferred_element_type=jnp.float32)
              + jnp.einsum('bqhr,bkr->bhqk', qr, k_rope, preferred_element_type=jnp.float32))
        p = jax.nn.softmax(sc * scale, axis=-1).astype(v.dtype)
        return jnp.einsum('bhqk,bkhd->bqhd', p, v)

    out = lax.map(one_block, (qn_blocks, qr_blocks))
    return out.transpose(1, 0, 2, 3, 4).reshape(b, s, MLA_WIDTH)


def hierarchical_moe(h, wg, bg, we, be, w1, w3, w2):
    b, s, d = h.shape
    t = b * s
    hf = h.reshape(t, d)
    g_prob = jax.nn.softmax((hf @ wg).astype(jnp.float32) + bg, axis=-1)
    g_top, g_idx = lax.top_k(g_prob, 1)
    e_logits = ((hf @ we).astype(jnp.float32) + be).reshape(t, N_GROUPS, EXPERTS_PER_GROUP)
    e_logits = jnp.take_along_axis(e_logits, g_idx[:, :, None], axis=1)[:, 0]
    e_prob = jax.nn.softmax(e_logits, axis=-1)
    e_top, e_idx = lax.top_k(e_prob, TOP_K)
    weights = g_top * e_top / jnp.sum(e_top, axis=-1, keepdims=True)
    expert_id = (g_idx * EXPERTS_PER_GROUP + e_idx).reshape(-1)
    token_id = jnp.repeat(jnp.arange(t), TOP_K)
    w_flat = weights.reshape(-1)
    n_assign = t * TOP_K

    order = jnp.argsort(expert_id)
    sorted_e = expert_id[order]
    sorted_tok = token_id[order]
    counts = jnp.zeros((N_EXPERTS,), jnp.int32).at[expert_id].add(1)
    starts = jnp.cumsum(counts) - counts
    padded = (counts + MOE_BLOCK - 1) // MOE_BLOCK * MOE_BLOCK
    pad_ends = jnp.cumsum(padded)
    pad_starts = pad_ends - padded
    dest = pad_starts[sorted_e] + jnp.arange(n_assign) - starts[sorted_e]
    n_blocks = (n_assign + N_EXPERTS * (MOE_BLOCK - 1) + MOE_BLOCK - 1) // MOE_BLOCK
    buf = jnp.zeros((n_blocks * MOE_BLOCK, d), h.dtype).at[dest].set(hf[sorted_tok])
    block_expert = jnp.minimum(
        jnp.searchsorted(pad_ends, jnp.arange(n_blocks) * MOE_BLOCK, side='right'), N_EXPERTS - 1)

    def expert_block(args):
        xb, e = args
        return (jax.nn.silu(xb @ w1[e]) * (xb @ w3[e])) @ w2[e]

    y = lax.map(expert_block, (buf.reshape(n_blocks, MOE_BLOCK, d), block_expert)).reshape(-1, d)
    y_assign = y[dest] * w_flat[order][:, None].astype(y.dtype)
    out = jnp.zeros((t, d), y.dtype).at[sorted_tok].add(y_assign)
    return out.reshape(b, s, d)


def trunk(x, c, ada_w, ada_b, norm_mix_g, w_in, mla_q_norm_g, w_uq, mla_kv_norm_g, w_ukv,
          na_rpb, w_na_o, w_mla_o, w_out, norm_ffn_g, router_wg, router_bg, router_we,
          router_be, expert_w1, expert_w3, expert_w2, final_norm_g):
    b, s, _ = x.shape
    for l in range(DEPTH):
        mod = (jax.nn.silu(c) @ ada_w[l] + ada_b[l])[:, None, :]
        sh1, sc1, g1, sh2, sc2, g2 = jnp.split(mod, 6, axis=-1)
        h = rms_norm(x, norm_mix_g[l]) * (1 + sc1) + sh1
        proj = h @ w_in[l]
        q_na, k_na, v_na, c_q, c_kv, k_rope, gate_logits = jnp.split(proj, IN_SPLITS, axis=-1)
        y_na = neighbourhood_attention(
            q_na.reshape(b, s, NA_HEADS, NA_HEAD_DIM),
            k_na.reshape(b, s, NA_HEADS, NA_HEAD_DIM),
            v_na.reshape(b, s, NA_HEADS, NA_HEAD_DIM), na_rpb[l]) @ w_na_o[l]
        y_mla = mla_attention(c_q, c_kv, k_rope, mla_q_norm_g[l], w_uq[l],
                              mla_kv_norm_g[l], w_ukv[l]) @ w_mla_o[l]
        g_na, g_mla = jnp.split(jax.nn.sigmoid(gate_logits), 2, axis=-1)
        x = x + g1 * ((g_na * y_na + g_mla * y_mla) @ w_out[l])
        h2 = rms_norm(x, norm_ffn_g[l]) * (1 + sc2) + sh2
        x = x + g2 * hierarchical_moe(h2, router_wg[l], router_bg[l], router_we[l], router_be[l],
                                      expert_w1[l], expert_w3[l], expert_w2[l])
    return rms_norm(x, final_norm_g)


def setup_inputs(seed: int = 0) -> dict:
    key = jax.random.key(seed)
    ks = jax.random.split(key, 32)
    D = D_MODEL

    def nrm(k, shape, scale):
        return jax.random.normal(k, shape, jnp.float32) * scale

    def gain(k, shape):
        return 1.0 + 0.01 * jax.random.normal(k, shape, jnp.float32)

    return {
        "x_prompt": nrm(ks[0], (BATCH, SEQ, D), 1.0),
        "x_sample": nrm(ks[1], (DEC_BATCH, DEC_SEQ, D), 1.0),
        "c_prompt": nrm(ks[2], (BATCH, D), 1.0),
        "c_sample": nrm(ks[3], (DEC_BATCH, D), 1.0),
        "ada_w": nrm(ks[4], (DEPTH, D, 6 * D), 0.5 * D ** -0.5),
        "ada_b": nrm(ks[5], (DEPTH, 6 * D), 0.02),
        "norm_mix_g": gain(ks[6], (DEPTH, D)),
        "w_in": nrm(ks[7], (DEPTH, D, IN_WIDTH), D ** -0.5),
        "mla_q_norm_g": gain(ks[8], (DEPTH, MLA_Q_RANK)),
        "w_uq": nrm(ks[9], (DEPTH, MLA_Q_RANK, MLA_HEADS * (MLA_NOPE_DIM + MLA_ROPE_DIM)), MLA_Q_RANK ** -0.5),
        "mla_kv_norm_g": gain(ks[10], (DEPTH, MLA_KV_RANK)),
        "w_ukv": nrm(ks[11], (DEPTH, MLA_KV_RANK, MLA_HEADS * (MLA_NOPE_DIM + MLA_V_DIM)), MLA_KV_RANK ** -0.5),
        "na_rpb": nrm(ks[12], (DEPTH, NA_HEADS, 2 * NA_KH_MAX - 1, 2 * NA_KW - 1), 0.1),
        "w_na_o": nrm(ks[13], (DEPTH, NA_WIDTH, D), NA_WIDTH ** -0.5),
        "w_mla_o": nrm(ks[14], (DEPTH, MLA_WIDTH, D), MLA_WIDTH ** -0.5),
        "w_out": nrm(ks[15], (DEPTH, D, D), D ** -0.5),
        "norm_ffn_g": gain(ks[16], (DEPTH, D)),
        "router_wg": nrm(ks[17], (DEPTH, D, N_GROUPS), D ** -0.5),
        "router_bg": nrm(ks[18], (DEPTH, N_GROUPS), 0.01),
        "router_we": nrm(ks[19], (DEPTH, D, N_EXPERTS), D ** -0.5),
        "router_be": nrm(ks[20], (DEPTH, N_EXPERTS), 0.01),
        "expert_w1": nrm(ks[21], (DEPTH, N_EXPERTS, D, D_EXPERT), D ** -0.5),
        "expert_w3": nrm(ks[22], (DEPTH, N_EXPERTS, D, D_EXPERT), D ** -0.5),
        "expert_w2": nrm(ks[23], (DEPTH, N_EXPERTS, D_EXPERT, D), D_EXPERT ** -0.5),
        "final_norm_g": gain(ks[24], (D,)),
    }


def reference(x_prompt, x_sample, c_prompt, c_sample, ada_w, ada_b, norm_mix_g, w_in,
              mla_q_norm_g, w_uq, mla_kv_norm_g, w_ukv, na_rpb, w_na_o, w_mla_o, w_out,
              norm_ffn_g, router_wg, router_bg, router_we, router_be, expert_w1, expert_w3,
              expert_w2, final_norm_g):
    y_prompt = trunk(x_prompt, c_prompt, ada_w, ada_b, norm_mix_g, w_in, mla_q_norm_g, w_uq,
                     mla_kv_norm_g, w_ukv, na_rpb, w_na_o, w_mla_o, w_out, norm_ffn_g,
                     router_wg, router_bg, router_we, router_be, expert_w1, expert_w3,
                     expert_w2, final_norm_g)
    y_sample = trunk(x_sample, c_sample, ada_w, ada_b, norm_mix_g, w_in, mla_q_norm_g, w_uq,
                     mla_kv_norm_g, w_ukv, na_rpb, w_na_o, w_mla_o, w_out, norm_ffn_g,
                     router_wg, router_bg, router_we, router_be, expert_w1, expert_w3,
                     expert_w2, final_norm_g)
    return (y_prompt, y_sample)
```

```python
import functools
import math

import numpy as np
import jax
import jax.numpy as jnp
from jax import lax
from jax.experimental import pallas as pl
from jax.experimental.pallas import tpu as pltpu

D_MODEL = 1024
DEPTH = 4
GRID_W = 64
NA_HEADS = 8
NA_HEAD_DIM = 64
NA_WIDTH = NA_HEADS * NA_HEAD_DIM
NA_KH = 8
NA_KW = 16
MLA_HEADS = 8
MLA_NOPE = 64
MLA_ROPE = 32
MLA_V = 64
MLA_Q_RANK = 256
MLA_KV_RANK = 128
ROPE_THETA = 10000.0
N_GROUPS = 4
EXPERTS_PER_GROUP = 8
N_EXPERTS = N_GROUPS * EXPERTS_PER_GROUP
D_EXPERT = 512
EPS = 1e-6

LANES = 128
HEAD_PAD = LANES
MLA_PAD_W = MLA_HEADS * HEAD_PAD
NEG_BIG = -1e30
VMEM_LIMIT = 56 * 1024 * 1024

TM = 256
TQ = 512
TK = 1024
NA_BAND = 8
BM = 256
SEQ_PAD = 16

SEC_NA = 3 * NA_WIDTH
SEC_LAT = MLA_Q_RANK + MLA_KV_RANK + 2 * HEAD_PAD
SEC_GATE = 2 * D_MODEL
W_IN_P = SEC_NA + SEC_LAT + SEC_GATE


def _cparams(sem, limit=VMEM_LIMIT):
    return pltpu.CompilerParams(dimension_semantics=sem, vmem_limit_bytes=limit)


def _bf(x):
    return x.astype(jnp.bfloat16)


def _dot(a, b):
    return jnp.dot(a, b, preferred_element_type=jnp.float32)


def _dot_t(a, b):
    return lax.dot_general(a, b, (((1,), (1,)), ((), ())), preferred_element_type=jnp.float32)


def _mod_kernel(c_ref, w_ref, b_ref, o_ref):
    c = c_ref[...]
    sc = c * jax.nn.sigmoid(c)
    o_ref[0] = _dot(_bf(sc), _bf(w_ref[0])) + b_ref[0]


def _modulation(c_all, ada_w, ada_b):
    depth, d, n = ada_w.shape
    tn = 1536
    return pl.pallas_call(
        _mod_kernel,
        grid=(depth, n // tn),
        in_specs=[
            pl.BlockSpec((SEQ_PAD, d), lambda l, j: (0, 0)),
            pl.BlockSpec((1, d, tn), lambda l, j: (l, 0, j)),
            pl.BlockSpec((1, 1, tn), lambda l, j: (l, 0, j)),
        ],
        out_specs=pl.BlockSpec((1, SEQ_PAD, tn), lambda l, j: (l, 0, j)),
        out_shape=jax.ShapeDtypeStruct((depth, SEQ_PAD, n), jnp.float32),
        compiler_params=_cparams(("arbitrary", "arbitrary")),
        name="modulation",
    )(c_all, ada_w, ada_b.reshape(depth, 1, n))


def _rms(x):
    return x * lax.rsqrt(jnp.mean(x * x, axis=-1, keepdims=True) + EPS)


def _premix_kernel(seq_ref, pos_ref, x_ref, mod_ref, g_ref, win_ref, gq_ref, wuq_ref, gkv_ref,
                   wukv_ref, rc_ref, rs_ref, vone_ref,
                   qna_ref, kna_ref, vna_ref, qm_ref, km_ref, vm_ref, gate_ref):
    i = pl.program_id(0)
    sid = seq_ref[i]
    d = D_MODEL
    sh1 = mod_ref[pl.ds(sid, 1), 0:d]
    sc1 = mod_ref[pl.ds(sid, 1), d:2 * d]
    x = x_ref[...]
    h = _bf((_rms(x) * g_ref[...]) * (1.0 + sc1) + sh1)

    na = _dot(h, win_ref[:, 0:SEC_NA])
    qna_ref[...] = _bf(na[:, 0:NA_WIDTH])
    kna_ref[...] = _bf(na[:, NA_WIDTH:2 * NA_WIDTH])
    vna_ref[...] = _bf(na[:, 2 * NA_WIDTH:3 * NA_WIDTH])

    gate_ref[...] = _bf(jax.nn.sigmoid(_dot(h, win_ref[:, SEC_NA + SEC_LAT:W_IN_P])))

    lat = _dot(h, win_ref[:, SEC_NA:SEC_NA + SEC_LAT])
    o_kv = MLA_Q_RANK
    o_ka = o_kv + MLA_KV_RANK
    o_kb = o_ka + HEAD_PAD
    cqn = _bf(_rms(lat[:, 0:o_kv]) * gq_ref[...])
    ckvn = _bf(_rms(lat[:, o_kv:o_ka]) * gkv_ref[...])
    rc = rc_ref[...]
    rs = rs_ref[...]
    qscale = (MLA_NOPE + MLA_ROPE) ** -0.5

    qa = _dot(cqn, wuq_ref[...])
    kv = _dot(ckvn, wukv_ref[...])
    kr = lat[:, o_ka:o_kb] * rc + lat[:, o_kb:o_kb + HEAD_PAD] * rs
    for hd in range(MLA_HEADS):
        lo = hd * HEAD_PAD
        hi = lo + HEAD_PAD
        qh = (qa[:, lo:hi] * rc + qa[:, MLA_PAD_W + lo:MLA_PAD_W + hi] * rs) * qscale
        qm_ref[:, lo:hi] = _bf(qh)
        km_ref[:, lo:hi] = _bf(kv[:, lo:hi] + kr)
    vm_ref[...] = _bf(kv[:, MLA_PAD_W:2 * MLA_PAD_W] + vone_ref[...])


def _premix(x, mod_l, lw, tabs, meta):
    t = x.shape[0]
    nt = t // TM
    row = lambda i, s, p: (i, 0)
    full = lambda i, s, p: (0, 0)
    bf = jnp.bfloat16
    outs = [jax.ShapeDtypeStruct((t, NA_WIDTH), bf)] * 3 + [jax.ShapeDtypeStruct((t, MLA_PAD_W), bf)] * 3 \
        + [jax.ShapeDtypeStruct((t, SEC_GATE), bf)]
    out_specs = [pl.BlockSpec((TM, NA_WIDTH), row)] * 3 + [pl.BlockSpec((TM, MLA_PAD_W), row)] * 3 \
        + [pl.BlockSpec((TM, SEC_GATE), row)]
    grid_spec = pltpu.PrefetchScalarGridSpec(
        num_scalar_prefetch=2,
        grid=(nt,),
        in_specs=[
            pl.BlockSpec((TM, D_MODEL), row),
            pl.BlockSpec((SEQ_PAD, 6 * D_MODEL), full),
            pl.BlockSpec((1, D_MODEL), full),
            pl.BlockSpec((D_MODEL, W_IN_P), full),
            pl.BlockSpec((1, MLA_Q_RANK), full),
            pl.BlockSpec((MLA_Q_RANK, 2 * MLA_PAD_W), full),
            pl.BlockSpec((1, MLA_KV_RANK), full),
            pl.BlockSpec((MLA_KV_RANK, 2 * MLA_PAD_W), full),
            pl.BlockSpec((TM, HEAD_PAD), lambda i, s, p: (p[i], 0)),
            pl.BlockSpec((TM, HEAD_PAD), lambda i, s, p: (p[i], 0)),
            pl.BlockSpec((1, MLA_PAD_W), full),
        ],
        out_specs=out_specs,
    )
    return pl.pallas_call(
        _premix_kernel, grid_spec=grid_spec, out_shape=outs,
        compiler_params=_cparams(("arbitrary",)), name="premix",
    )(meta["tile_seq"], meta["tile_pos"], x, mod_l, lw["g_mix"], lw["w_in"], lw["g_q"], lw["w_uq"],
      lw["g_kv"], lw["w_ukv"], tabs["rope_c"], tabs["rope_s"], tabs["v_ones"])


def _na_kernel(row0_ref, nrow_ref, q_ref, kp_ref, kc_ref, kn_ref, vp_ref, vc_ref, vn_ref, bias_ref,
               o_ref, kcat, vcat):
    j = pl.program_id(0)
    band_tok = NA_BAND * GRID_W
    kcat[0:band_tok] = kp_ref[...]
    kcat[band_tok:2 * band_tok] = kc_ref[...]
    kcat[2 * band_tok:3 * band_tok] = kn_ref[...]
    vcat[0:band_tok] = vp_ref[...]
    vcat[band_tok:2 * band_tok] = vc_ref[...]
    vcat[2 * band_tok:3 * band_tok] = vn_ref[...]
    seq_row0 = row0_ref[j]
    seq_rows = nrow_ref[j]
    lane = lax.broadcasted_iota(jnp.int32, (1, LANES), 1)
    low_half = lane < NA_HEAD_DIM
    win_tok = NA_KH * GRID_W

    def one_row(i, carry):
        r_loc = j * NA_BAND + i - seq_row0
        rs = jnp.clip(r_loc - NA_KH // 2, 0, seq_rows - NA_KH)
        start = NA_BAND + seq_row0 + rs - j * NA_BAND
        first_dr = (NA_KH - 1) - (r_loc - rs)
        k_off = pl.multiple_of(start * GRID_W, GRID_W)
        q_off = pl.multiple_of(i * GRID_W, GRID_W)
        kw = kcat[pl.ds(k_off, win_tok), :]
        vw = vcat[pl.ds(k_off, win_tok), :]
        qrow = q_ref[pl.ds(q_off, GRID_W), :]
        outs = []
        for hp in range(NA_HEADS // 2):
            sl = slice(hp * LANES, (hp + 1) * LANES)
            q2 = qrow[:, sl]
            k2 = kw[:, sl]
            v2 = vw[:, sl]
            o_pair = None
            for half in range(2):
                hd = 2 * hp + half
                keep = low_half if half == 0 else jnp.logical_not(low_half)
                km = jnp.where(keep, k2, jnp.zeros_like(k2))
                s = _dot_t(q2, km) + bias_ref[hd * NA_KH + first_dr]
                m = jnp.max(s, axis=-1, keepdims=True)
                p = jnp.exp(s - m)
                l = jnp.sum(p, axis=-1, keepdims=True)
                o = _dot(_bf(p), v2) / l
                o_pair = o if o_pair is None else jnp.where(low_half, o_pair, o)
            outs.append(o_pair)
        o_ref[pl.ds(q_off, GRID_W), :] = _bf(jnp.concatenate(outs, axis=-1))
        return carry

    lax.fori_loop(0, NA_BAND, one_row, 0)


def _na_attention(q, k, v, bias_tab, meta):
    t = q.shape[0]
    band_tok = NA_BAND * GRID_W
    nb = t // band_tok
    cur = lambda j, r0, nr, pb, nx: (j, 0)
    prv = lambda j, r0, nr, pb, nx: (pb[j], 0)
    nxt = lambda j, r0, nr, pb, nx: (nx[j], 0)
    blk = lambda f: pl.BlockSpec((band_tok, NA_WIDTH), f)

    def kern(r0, nr, pb, nx, *refs):
        _na_kernel(r0, nr, *refs)

    grid_spec = pltpu.PrefetchScalarGridSpec(
        num_scalar_prefetch=4,
        grid=(nb,),
        in_specs=[blk(cur), blk(prv), blk(cur), blk(nxt), blk(prv), blk(cur), blk(nxt),
                  pl.BlockSpec((NA_HEADS * NA_KH, GRID_W, NA_KH * GRID_W), lambda j, r0, nr, pb, nx: (0, 0, 0))],
        out_specs=blk(cur),
        scratch_shapes=[pltpu.VMEM((3 * band_tok, NA_WIDTH), jnp.bfloat16),
                        pltpu.VMEM((3 * band_tok, NA_WIDTH), jnp.bfloat16)],
    )
    return pl.pallas_call(
        kern, grid_spec=grid_spec, out_shape=jax.ShapeDtypeStruct((t, NA_WIDTH), jnp.bfloat16),
        compiler_params=_cparams(("arbitrary",)), name="na_attention",
    )(meta["band_row0"], meta["band_nrow"], meta["band_prev"], meta["band_next"], q, k, k, k, v, v, v, bias_tab)


def _mla_kernel(qi_ref, kj_ref, first_ref, last_ref, q_ref, k_ref, v_ref, o_ref, m_scr, acc_scr):
    n = pl.program_id(0)

    @pl.when(first_ref[n] == 1)
    def _():
        m_scr[...] = jnp.full(m_scr.shape, NEG_BIG, jnp.float32)
        acc_scr[...] = jnp.zeros(acc_scr.shape, jnp.float32)

    for hd in range(MLA_HEADS):
        sl = slice(hd * HEAD_PAD, (hd + 1) * HEAD_PAD)
        s = _dot_t(q_ref[:, sl], k_ref[:, sl])
        m_prev = m_scr[hd]
        m_new = jnp.maximum(m_prev, jnp.max(s, axis=-1, keepdims=True))
        p = jnp.exp(s - m_new[:, 0:1])
        alpha = jnp.exp(m_prev - m_new)
        acc_scr[hd] = alpha * acc_scr[hd] + _dot(_bf(p), v_ref[:, sl])
        m_scr[hd] = m_new

    @pl.when(last_ref[n] == 1)
    def _():
        for hd in range(MLA_HEADS):
            a = acc_scr[hd]
            o_ref[:, hd * HEAD_PAD:(hd + 1) * HEAD_PAD] = _bf(a / a[:, MLA_V:MLA_V + 1])


def _mla_attention(qm, km, vm, meta):
    t = qm.shape[0]
    n_pairs = meta["mla_qi"].shape[0]
    grid_spec = pltpu.PrefetchScalarGridSpec(
        num_scalar_prefetch=4,
        grid=(n_pairs,),
        in_specs=[
            pl.BlockSpec((TQ, MLA_PAD_W), lambda n, qi, kj, f, l: (qi[n], 0)),
            pl.BlockSpec((TK, MLA_PAD_W), lambda n, qi, kj, f, l: (kj[n], 0)),
            pl.BlockSpec((TK, MLA_PAD_W), lambda n, qi, kj, f, l: (kj[n], 0)),
        ],
        out_specs=pl.BlockSpec((TQ, MLA_PAD_W), lambda n, qi, kj, f, l: (qi[n], 0)),
        scratch_shapes=[pltpu.VMEM((MLA_HEADS, TQ, HEAD_PAD), jnp.float32),
                        pltpu.VMEM((MLA_HEADS, TQ, HEAD_PAD), jnp.float32)],
    )
    return pl.pallas_call(
        _mla_kernel, grid_spec=grid_spec, out_shape=jax.ShapeDtypeStruct((t, MLA_PAD_W), jnp.bfloat16),
        compiler_params=_cparams(("arbitrary",)), name="mla_attention",
    )(meta["mla_qi"], meta["mla_kj"], meta["mla_first"], meta["mla_last"], qm, km, vm)


def _postmix_kernel(seq_ref, x_ref, yna_ref, ymla_ref, gate_ref, mod_ref, wna_ref, wmla_ref, wout_ref,
                    g_ref, wr_hi_ref, wr_lo_ref, br_ref,
                    xo_ref, h2_ref, eid_ref, wt_ref):
    i = pl.program_id(0)
    sid = seq_ref[i]
    d = D_MODEL
    g1 = mod_ref[pl.ds(sid, 1), 2 * d:3 * d]
    sh2 = mod_ref[pl.ds(sid, 1), 3 * d:4 * d]
    sc2 = mod_ref[pl.ds(sid, 1), 4 * d:5 * d]
    a = _dot(yna_ref[...], wna_ref[...])
    b = _dot(ymla_ref[...], wmla_ref[...])
    gate = gate_ref[...]
    merged = gate[:, 0:d].astype(jnp.float32) * a + gate[:, d:2 * d].astype(jnp.float32) * b
    xn = x_ref[...] + g1 * _dot(_bf(merged), wout_ref[...])
    xo_ref[...] = xn
    h2 = (_rms(xn) * g_ref[...]) * (1.0 + sc2) + sh2
    h2_ref[...] = h2

    h_hi = _bf(h2)
    h_lo = _bf(h2 - h_hi.astype(jnp.float32))
    logits = (_dot(h_hi, wr_hi_ref[...]) + _dot(h_hi, wr_lo_ref[...]) + _dot(h_lo, wr_hi_ref[...])) + br_ref[...]
    lane = lax.broadcasted_iota(jnp.int32, logits.shape, 1)
    is_g = lane < N_GROUPS
    gl = jnp.where(is_g, logits, NEG_BIG)
    gmax = jnp.max(gl, axis=-1, keepdims=True)
    gsum = jnp.sum(jnp.where(is_g, jnp.exp(gl - gmax), 0.0), axis=-1, keepdims=True)
    g_top = 1.0 / gsum
    g_idx = jnp.min(jnp.where(gl == gmax, lane, LANES), axis=-1, keepdims=True)
    e_lo = N_GROUPS + g_idx * EXPERTS_PER_GROUP
    in_grp = (lane >= e_lo) & (lane < e_lo + EXPERTS_PER_GROUP)
    el = jnp.where(in_grp, logits, NEG_BIG)
    m1 = jnp.max(el, axis=-1, keepdims=True)
    ep = jnp.where(in_grp, jnp.exp(el - m1), 0.0)
    ep = ep / jnp.sum(ep, axis=-1, keepdims=True)
    p1 = jnp.max(ep, axis=-1, keepdims=True)
    i1 = jnp.min(jnp.where(in_grp & (ep == p1), lane, LANES), axis=-1, keepdims=True)
    rest = in_grp & (lane != i1)
    p2 = jnp.max(jnp.where(rest, ep, -1.0), axis=-1, keepdims=True)
    i2 = jnp.min(jnp.where(rest & (ep == p2), lane, LANES), axis=-1, keepdims=True)
    denom = p1 + p2
    w1 = g_top * p1 / denom
    w2 = g_top * p2 / denom
    eid_ref[...] = jnp.where(lane == 0, i1 - N_GROUPS, jnp.where(lane == 1, i2 - N_GROUPS, 0))
    wt_ref[...] = jnp.where(lane == 0, w1, jnp.where(lane == 1, w2, 0.0))


def _postmix(x, yna, ymla, gates, mod_l, lw, meta):
    t = x.shape[0]
    nt = t // TM
    row = lambda i, s: (i, 0)
    full = lambda i, s: (0, 0)
    d = D_MODEL
    grid_spec = pltpu.PrefetchScalarGridSpec(
        num_scalar_prefetch=1,
        grid=(nt,),
        in_specs=[
            pl.BlockSpec((TM, d), row),
            pl.BlockSpec((TM, NA_WIDTH), row),
            pl.BlockSpec((TM, MLA_PAD_W), row),
            pl.BlockSpec((TM, SEC_GATE), row),
            pl.BlockSpec((SEQ_PAD, 6 * d), full),
            pl.BlockSpec((NA_WIDTH, d), full),
            pl.BlockSpec((MLA_PAD_W, d), full),
            pl.BlockSpec((d, d), full),
            pl.BlockSpec((1, d), full),
            pl.BlockSpec((d, LANES), full),
            pl.BlockSpec((d, LANES), full),
            pl.BlockSpec((1, LANES), full),
        ],
        out_specs=[pl.BlockSpec((TM, d), row), pl.BlockSpec((TM, d), row),
                   pl.BlockSpec((TM, LANES), row), pl.BlockSpec((TM, LANES), row)],
    )
    return pl.pallas_call(
        _postmix_kernel, grid_spec=grid_spec,
        out_shape=[jax.ShapeDtypeStruct((t, d), jnp.float32), jax.ShapeDtypeStruct((t, d), jnp.float32),
                   jax.ShapeDtypeStruct((t, LANES), jnp.int32), jax.ShapeDtypeStruct((t, LANES), jnp.float32)],
        compiler_params=_cparams(("arbitrary",)), name="postmix",
    )(meta["tile_seq"], x, yna, ymla, gates, mod_l, lw["w_na_o"], lw["w_mla_o"], lw["w_out"], lw["g_ffn"],
      lw["w_r_hi"], lw["w_r_lo"], lw["b_r"])


def _expert_kernel(be_ref, nv_ref, tok_ref, slot_ref, h_hbm, w1_ref, w3_ref, w2_ref, y_hbm,
                   xbuf, ybuf, sem_in, sem_out):
    j = pl.program_id(0)
    nv = nv_ref[j]

    @pl.when(j == 0)
    def _():
        xbuf[...] = jnp.zeros(xbuf.shape, xbuf.dtype)

    def gather_copy(r):
        return pltpu.make_async_copy(h_hbm.at[pl.ds(tok_ref[0, 0, r], 1)], xbuf.at[pl.ds(r, 1)], sem_in)

    def scatter_copy(r):
        return pltpu.make_async_copy(ybuf.at[pl.ds(r, 1)], y_hbm.at[pl.ds(slot_ref[0, 0, r], 1)], sem_out)

    @pl.when(nv > 0)
    def _():
        def start_in(r, c):
            gather_copy(r).start()
            return c

        def wait_in(r, c):
            gather_copy(r).wait()
            return c

        lax.fori_loop(0, nv, start_in, 0)
        lax.fori_loop(0, nv, wait_in, 0)
        xb = _bf(xbuf[...])
        a = _dot(xb, w1_ref[0])
        b = _dot(xb, w3_ref[0])
        hmid = _bf((a * jax.nn.sigmoid(a)) * b)
        ybuf[...] = _dot(hmid, w2_ref[0])

        def start_out(r, c):
            scatter_copy(r).start()
            return c

        def wait_out(r, c):
            scatter_copy(r).wait()
            return c

        lax.fori_loop(0, nv, start_out, 0)
        lax.fori_loop(0, nv, wait_out, 0)


def _experts(h2, lw, disp, n_assign):
    d = D_MODEL
    n_blocks = disp["block_expert"].shape[0]
    grid_spec = pltpu.PrefetchScalarGridSpec(
        num_scalar_prefetch=2,
        grid=(n_blocks,),
        in_specs=[
            pl.BlockSpec((1, 1, BM), lambda j, be, nv: (j, 0, 0), memory_space=pltpu.SMEM),
            pl.BlockSpec((1, 1, BM), lambda j, be, nv: (j, 0, 0), memory_space=pltpu.SMEM),
            pl.BlockSpec(memory_space=pl.ANY),
            pl.BlockSpec((1, d, D_EXPERT), lambda j, be, nv: (be[j], 0, 0)),
            pl.BlockSpec((1, d, D_EXPERT), lambda j, be, nv: (be[j], 0, 0)),
            pl.BlockSpec((1, D_EXPERT, d), lambda j, be, nv: (be[j], 0, 0)),
        ],
        out_specs=pl.BlockSpec(memory_space=pl.ANY),
        scratch_shapes=[pltpu.VMEM((BM, d), jnp.float32), pltpu.VMEM((BM, d), jnp.float32),
                        pltpu.SemaphoreType.DMA(()), pltpu.SemaphoreType.DMA(())],
    )
    return pl.pallas_call(
        _expert_kernel, grid_spec=grid_spec, out_shape=jax.ShapeDtypeStruct((n_assign, d), jnp.float32),
        compiler_params=_cparams(("arbitrary",)), name="experts",
    )(disp["block_expert"], disp["block_nvalid"], disp["src_tok"], disp["dst_slot"], h2,
      lw["w1"], lw["w3"], lw["w2"])


def _dispatch(eid):
    t = eid.shape[0]
    a = 2 * t
    e_flat = eid.reshape(a)
    order = jnp.argsort(e_flat, stable=True).astype(jnp.int32)
    counts = jnp.sum((e_flat[:, None] == jnp.arange(N_EXPERTS, dtype=jnp.int32)[None, :]).astype(jnp.int32), axis=0)
    starts = jnp.cumsum(counts) - counts
    padded = (counts + BM - 1) // BM * BM
    pad_ends = jnp.cumsum(padded)
    pad_starts = pad_ends - padded
    n_blocks = (a + N_EXPERTS * (BM - 1) + BM - 1) // BM
    blk0 = jnp.arange(n_blocks, dtype=jnp.int32) * BM
    block_expert = jnp.minimum(jnp.searchsorted(pad_ends, blk0, side="right"), N_EXPERTS - 1).astype(jnp.int32)
    in_expert = blk0 - pad_starts[block_expert]
    block_nvalid = jnp.clip(counts[block_expert] - in_expert, 0, BM).astype(jnp.int32)
    pos = in_expert[:, None] + jnp.arange(BM, dtype=jnp.int32)[None, :]
    sorted_pos = jnp.clip(starts[block_expert][:, None] + pos, 0, a - 1)
    slot = order[sorted_pos]
    return {
        "block_expert": block_expert,
        "block_nvalid": block_nvalid,
        "src_tok": (slot // 2).reshape(n_blocks, 1, BM),
        "dst_slot": slot.reshape(n_blocks, 1, BM),
    }


def _combine_kernel(seq_ref, x_ref, y_ref, wt_ref, mod_ref, gf_ref, o_ref, *, final):
    i = pl.program_id(0)
    sid = seq_ref[i]
    d = D_MODEL
    g2 = mod_ref[pl.ds(sid, 1), 5 * d:6 * d]
    wt = wt_ref[...]
    moe = y_ref[:, 0:d] * wt[:, 0:1] + y_ref[:, d:2 * d] * wt[:, 1:2]
    xn = x_ref[...] + g2 * moe
    if final:
        xn = _rms(xn) * gf_ref[...]
    o_ref[...] = xn


def _combine(x, y_tok, wts, mod_l, g_final, meta, final):
    t = x.shape[0]
    nt = t // TM
    d = D_MODEL
    row = lambda i, s: (i, 0)
    full = lambda i, s: (0, 0)
    grid_spec = pltpu.PrefetchScalarGridSpec(
        num_scalar_prefetch=1,
        grid=(nt,),
        in_specs=[pl.BlockSpec((TM, d), row), pl.BlockSpec((TM, 2 * d), row), pl.BlockSpec((TM, LANES), row),
                  pl.BlockSpec((SEQ_PAD, 6 * d), full), pl.BlockSpec((1, d), full)],
        out_specs=pl.BlockSpec((TM, d), row),
    )
    return pl.pallas_call(
        functools.partial(_combine_kernel, final=final), grid_spec=grid_spec,
        out_shape=jax.ShapeDtypeStruct((t, d), jnp.float32),
        compiler_params=_cparams(("arbitrary",)), name="combine",
    )(meta["tile_seq"], x, y_tok.reshape(t, 2 * d), wts, mod_l, g_final)


def _static_meta(seqs):
    tile_seq, tile_pos = [], []
    band_row0, band_nrow, band_prev, band_next = [], [], [], []
    qi, kj, first, last = [], [], [], []
    sid, tok0 = 0, 0
    band_tok = NA_BAND * GRID_W
    for b, s in seqs:
        assert s % TK == 0 and s % TQ == 0 and s % TM == 0 and s % band_tok == 0 and s // GRID_W >= NA_KH
        for _ in range(b):
            for i in range(s // TM):
                tile_seq.append(sid)
                tile_pos.append(i)
            nb, b0 = s // band_tok, tok0 // band_tok
            for j in range(nb):
                band_row0.append(tok0 // GRID_W)
                band_nrow.append(s // GRID_W)
                band_prev.append(b0 + max(j - 1, 0))
                band_next.append(b0 + min(j + 1, nb - 1))
            for i in range(s // TQ):
                for j in range(s // TK):
                    qi.append(tok0 // TQ + i)
                    kj.append(tok0 // TK + j)
                    first.append(int(j == 0))
                    last.append(int(j == s // TK - 1))
            sid += 1
            tok0 += s
    arr = lambda v: jnp.asarray(np.asarray(v, np.int32))
    return {"tile_seq": arr(tile_seq), "tile_pos": arr(tile_pos), "band_row0": arr(band_row0),
            "band_nrow": arr(band_nrow), "band_prev": arr(band_prev), "band_next": arr(band_next),
            "mla_qi": arr(qi), "mla_kj": arr(kj), "mla_first": arr(first), "mla_last": arr(last)}


def _tables(max_seq):
    inv = 1.0 / (ROPE_THETA ** (jnp.arange(0, MLA_ROPE, 2, dtype=jnp.float32) / MLA_ROPE))
    ang = jnp.arange(max_seq, dtype=jnp.float32)[:, None] * inv[None, :]
    cos, sin = jnp.cos(ang), jnp.sin(ang)
    ones = jnp.ones((max_seq, MLA_NOPE), jnp.float32)
    zpad = jnp.zeros((max_seq, HEAD_PAD - MLA_NOPE - MLA_ROPE), jnp.float32)
    rope_c = jnp.concatenate([ones, cos, cos, zpad], axis=1)
    rope_s = jnp.concatenate([0.0 * ones, sin, sin, zpad], axis=1)
    v_ones = jnp.zeros((MLA_HEADS, HEAD_PAD), jnp.float32).at[:, MLA_V].set(1.0).reshape(1, MLA_PAD_W)
    return {"rope_c": rope_c, "rope_s": rope_s, "v_ones": v_ones}


def _pad_heads(w, width):
    k = w.shape[0]
    w = w.reshape(k, MLA_HEADS, width)
    return jnp.pad(w, ((0, 0), (0, 0), (0, HEAD_PAD - width))).reshape(k, MLA_PAD_W)


def _rope_swap(w_rope):
    half = MLA_ROPE // 2
    return jnp.concatenate([-w_rope[..., half:], w_rope[..., :half]], axis=-1)


def _na_bias_table(rpb):
    col = jnp.arange(GRID_W)
    col_start = jnp.clip(col - NA_KW // 2, 0, GRID_W - NA_KW)
    kc = jnp.arange(GRID_W)
    inside = (kc[None, :] >= col_start[:, None]) & (kc[None, :] < col_start[:, None] + NA_KW)
    dc = jnp.clip(kc[None, :] - col[:, None] + NA_KW - 1, 0, 2 * NA_KW - 2)
    blocks = jnp.where(inside[None, None], rpb[:, :, dc], NEG_BIG)
    rows = []
    for first in range(NA_KH):
        win = blocks[:, first:first + NA_KH]
        rows.append(win.transpose(0, 2, 1, 3).reshape(NA_HEADS, GRID_W, NA_KH * GRID_W))
    return jnp.stack(rows, axis=1).reshape(NA_HEADS * NA_KH, GRID_W, NA_KH * GRID_W)


def _layer_weights(l, p):
    bf = jnp.bfloat16
    w_in = p["w_in"][l]
    o = 3 * NA_WIDTH
    w_q, w_k, w_v = w_in[:, 0:NA_WIDTH], w_in[:, NA_WIDTH:2 * NA_WIDTH], w_in[:, 2 * NA_WIDTH:o]
    w_cq = w_in[:, o:o + MLA_Q_RANK]
    w_ckv = w_in[:, o + MLA_Q_RANK:o + MLA_Q_RANK + MLA_KV_RANK]
    o2 = o + MLA_Q_RANK + MLA_KV_RANK
    w_kr = w_in[:, o2:o2 + MLA_ROPE]
    w_gate = w_in[:, o2 + MLA_ROPE:]
    zl = jnp.zeros((D_MODEL, MLA_NOPE), jnp.float32)
    zr = jnp.zeros((D_MODEL, HEAD_PAD - MLA_NOPE - MLA_ROPE), jnp.float32)
    kr_a = jnp.concatenate([zl, w_kr, zr], axis=1)
    kr_b = jnp.concatenate([zl, _rope_swap(w_kr), zr], axis=1)
    w_in_p = jnp.concatenate([w_q * (NA_HEAD_DIM ** -0.5), w_k, w_v, w_cq, w_ckv, kr_a, kr_b, w_gate], axis=1)

    w_uq = p["w_uq"][l].reshape(MLA_Q_RANK, MLA_HEADS, MLA_NOPE + MLA_ROPE)
    uq_plain = _pad_heads(w_uq.reshape(MLA_Q_RANK, -1), MLA_NOPE + MLA_ROPE)
    uq_swap = jnp.concatenate([jnp.zeros_like(w_uq[..., :MLA_NOPE]), _rope_swap(w_uq[..., MLA_NOPE:])], axis=-1)
    uq_swap = _pad_heads(uq_swap.reshape(MLA_Q_RANK, -1), MLA_NOPE + MLA_ROPE)
    w_ukv = p["w_ukv"][l].reshape(MLA_KV_RANK, MLA_HEADS, MLA_NOPE + MLA_V)
    uk = _pad_heads(w_ukv[..., :MLA_NOPE].reshape(MLA_KV_RANK, -1), MLA_NOPE)
    uv = _pad_heads(w_ukv[..., MLA_NOPE:].reshape(MLA_KV_RANK, -1), MLA_V)
    w_mla_o = p["w_mla_o"][l].reshape(MLA_HEADS, MLA_V, D_MODEL)
    w_mla_o = jnp.pad(w_mla_o, ((0, 0), (0, HEAD_PAD - MLA_V), (0, 0))).reshape(MLA_PAD_W, D_MODEL)

    w_r = jnp.concatenate([p["router_wg"][l], p["router_we"][l]], axis=1)
    w_r = jnp.pad(w_r, ((0, 0), (0, LANES - w_r.shape[1])))
    w_r_hi = w_r.astype(bf)
    b_r = jnp.concatenate([p["router_bg"][l], p["router_be"][l]])
    b_r = jnp.pad(b_r, (0, LANES - b_r.shape[0])).reshape(1, LANES)
    return {
        "g_mix": p["norm_mix_g"][l].reshape(1, D_MODEL),
        "w_in": w_in_p.astype(bf),
        "g_q": p["mla_q_norm_g"][l].reshape(1, MLA_Q_RANK),
        "w_uq": jnp.concatenate([uq_plain, uq_swap], axis=1).astype(bf),
        "g_kv": p["mla_kv_norm_g"][l].reshape(1, MLA_KV_RANK),
        "w_ukv": jnp.concatenate([uk, uv], axis=1).astype(bf),
        "na_bias": _na_bias_table(p["na_rpb"][l]),
        "w_na_o": p["w_na_o"][l].astype(bf),
        "w_mla_o": w_mla_o.astype(bf),
        "w_out": p["w_out"][l].astype(bf),
        "g_ffn": p["norm_ffn_g"][l].reshape(1, D_MODEL),
        "w_r_hi": w_r_hi,
        "w_r_lo": (w_r - w_r_hi.astype(jnp.float32)).astype(bf),
        "b_r": b_r,
        "w1": p["expert_w1"][l].astype(bf),
        "w3": p["expert_w3"][l].astype(bf),
        "w2": p["expert_w2"][l].astype(bf),
    }


def _forward(x_flat, c_all, seqs, p):
    t = x_flat.shape[0]
    n_seq = c_all.shape[0]
    assert n_seq <= SEQ_PAD
    meta = _static_meta(seqs)
    tabs = _tables(max(s for _, s in seqs))
    c_pad = jnp.pad(c_all, ((0, SEQ_PAD - n_seq), (0, 0)))
    mod = _modulation(c_pad, p["ada_w"], p["ada_b"])
    g_final = p["final_norm_g"].reshape(1, D_MODEL)
    x = x_flat
    for l in range(DEPTH):
        lw = _layer_weights(l, p)
        qna, kna, vna, qm, km, vm, gates = _premix(x, mod[l], lw, tabs, meta)
        yna = _na_attention(qna, kna, vna, lw["na_bias"], meta)
        ymla = _mla_attention(qm, km, vm, meta)
        x, h2, eid, wts = _postmix(x, yna, ymla, gates, mod[l], lw, meta)
        disp = _dispatch(eid[:, 0:2])
        y_tok = _experts(h2, lw, disp, 2 * t)
        x = _combine(x, y_tok, wts, mod[l], g_final, meta, final=(l == DEPTH - 1))
    return x


def kernel(x_prompt, x_sample, c_prompt, c_sample, ada_w, ada_b, norm_mix_g, w_in, mla_q_norm_g, w_uq,
           mla_kv_norm_g, w_ukv, na_rpb, w_na_o, w_mla_o, w_out, norm_ffn_g, router_wg, router_bg, router_we,
           router_be, expert_w1, expert_w3, expert_w2, final_norm_g):
    p = dict(ada_w=ada_w, ada_b=ada_b, norm_mix_g=norm_mix_g, w_in=w_in, mla_q_norm_g=mla_q_norm_g, w_uq=w_uq,
             mla_kv_norm_g=mla_kv_norm_g, w_ukv=w_ukv, na_rpb=na_rpb, w_na_o=w_na_o, w_mla_o=w_mla_o, w_out=w_out,
             norm_ffn_g=norm_ffn_g, router_wg=router_wg, router_bg=router_bg, router_we=router_we,
             router_be=router_be, expert_w1=expert_w1, expert_w3=expert_w3, expert_w2=expert_w2,
             final_norm_g=final_norm_g)
    bp, sp, d = x_prompt.shape
    bs, ss, _ = x_sample.shape
    x_flat = jnp.concatenate([x_prompt.reshape(bp * sp, d), x_sample.reshape(bs * ss, d)], axis=0)
    c_all = jnp.concatenate([c_prompt, c_sample], axis=0)
    y = _forward(x_flat, c_all, [(bp, sp), (bs, ss)], p)
    return y[:bp * sp].reshape(bp, sp, d), y[bp * sp:].reshape(bs, ss, d)
```

```python
import functools
import math

import numpy as np
import jax
import jax.numpy as jnp
from jax import lax
from jax.experimental import pallas as pl
from jax.experimental.pallas import tpu as pltpu

D_MODEL = 1024
DEPTH = 4
GRID_W = 64
NA_HEADS = 8
NA_HEAD_DIM = 64
NA_WIDTH = NA_HEADS * NA_HEAD_DIM
NA_KH = 8
NA_KW = 16
MLA_HEADS = 8
MLA_NOPE = 64
MLA_ROPE = 32
MLA_V = 64
MLA_Q_RANK = 256
MLA_KV_RANK = 128
ROPE_THETA = 10000.0
N_GROUPS = 4
EXPERTS_PER_GROUP = 8
N_EXPERTS = N_GROUPS * EXPERTS_PER_GROUP
D_EXPERT = 512
EPS = 1e-6

LANES = 128
HEAD_PAD = LANES
MLA_PAD_W = MLA_HEADS * HEAD_PAD
NEG_BIG = -1e30
VMEM_LIMIT = 56 * 1024 * 1024

TM = 256
TQ = 1024
TK = 1024
NA_BAND = 8
NA_UNION = 2 * NA_BAND
NA_MASKED = 2 * NA_KH - 1
NA_ENTRIES = NA_MASKED + 1
BM = 256
SEQ_PAD = 16

SEC_NA = 3 * NA_WIDTH
SEC_LAT = MLA_Q_RANK + MLA_KV_RANK + 2 * HEAD_PAD
SEC_GATE = 2 * D_MODEL
W_IN_P = SEC_NA + SEC_LAT + SEC_GATE


def _cparams(sem, limit=VMEM_LIMIT):
    return pltpu.CompilerParams(dimension_semantics=sem, vmem_limit_bytes=limit)


def _bf(x):
    return x.astype(jnp.bfloat16)


def _dot(a, b):
    return jnp.dot(a, b, preferred_element_type=jnp.float32)


def _dot_t(a, b):
    return lax.dot_general(a, b, (((1,), (1,)), ((), ())), preferred_element_type=jnp.float32)


def _mod_kernel(c_ref, w_ref, b_ref, o_ref):
    c = c_ref[...]
    sc = c * jax.nn.sigmoid(c)
    o_ref[0] = _dot(_bf(sc), _bf(w_ref[0])) + b_ref[0]


def _modulation(c_all, ada_w, ada_b):
    depth, d, n = ada_w.shape
    tn = 1536
    return pl.pallas_call(
        _mod_kernel,
        grid=(depth, n // tn),
        in_specs=[
            pl.BlockSpec((SEQ_PAD, d), lambda l, j: (0, 0)),
            pl.BlockSpec((1, d, tn), lambda l, j: (l, 0, j)),
            pl.BlockSpec((1, 1, tn), lambda l, j: (l, 0, j)),
        ],
        out_specs=pl.BlockSpec((1, SEQ_PAD, tn), lambda l, j: (l, 0, j)),
        out_shape=jax.ShapeDtypeStruct((depth, SEQ_PAD, n), jnp.float32),
        compiler_params=_cparams(("arbitrary", "arbitrary")),
        name="modulation",
    )(c_all, ada_w, ada_b.reshape(depth, 1, n))


def _rms(x):
    return x * lax.rsqrt(jnp.mean(x * x, axis=-1, keepdims=True) + EPS)


def _premix_kernel(seq_ref, pos_ref, x_ref, mod_ref, g_ref, win_ref, gq_ref, wuq_ref, gkv_ref,
                   wukv_ref, rc_ref, rs_ref, vone_ref,
                   qna_ref, kna_ref, vna_ref, qm_ref, km_ref, vm_ref, gate_ref):
    i = pl.program_id(0)
    sid = seq_ref[i]
    d = D_MODEL
    sh1 = mod_ref[pl.ds(sid, 1), 0:d]
    sc1 = mod_ref[pl.ds(sid, 1), d:2 * d]
    x = x_ref[...]
    h = _bf((_rms(x) * g_ref[...]) * (1.0 + sc1) + sh1)

    na = _dot(h, win_ref[:, 0:SEC_NA])
    qna_ref[...] = _bf(na[:, 0:NA_WIDTH])
    kna_ref[...] = _bf(na[:, NA_WIDTH:2 * NA_WIDTH])
    vna_ref[...] = _bf(na[:, 2 * NA_WIDTH:3 * NA_WIDTH])

    gate_ref[...] = _bf(jax.nn.sigmoid(_dot(h, win_ref[:, SEC_NA + SEC_LAT:W_IN_P])))

    lat = _dot(h, win_ref[:, SEC_NA:SEC_NA + SEC_LAT])
    o_kv = MLA_Q_RANK
    o_ka = o_kv + MLA_KV_RANK
    o_kb = o_ka + HEAD_PAD
    cqn = _bf(_rms(lat[:, 0:o_kv]) * gq_ref[...])
    ckvn = _bf(_rms(lat[:, o_kv:o_ka]) * gkv_ref[...])
    rc = rc_ref[...]
    rs = rs_ref[...]
    qscale = (MLA_NOPE + MLA_ROPE) ** -0.5 * math.log2(math.e)

    qa = _dot(cqn, wuq_ref[...])
    kv = _dot(ckvn, wukv_ref[...])
    kr = lat[:, o_ka:o_kb] * rc + lat[:, o_kb:o_kb + HEAD_PAD] * rs
    for hd in range(MLA_HEADS):
        lo = hd * HEAD_PAD
        hi = lo + HEAD_PAD
        qh = (qa[:, lo:hi] * rc + qa[:, MLA_PAD_W + lo:MLA_PAD_W + hi] * rs) * qscale
        qm_ref[:, lo:hi] = _bf(qh)
        km_ref[:, lo:hi] = _bf(kv[:, lo:hi] + kr)
    vm_ref[...] = _bf(kv[:, MLA_PAD_W:2 * MLA_PAD_W] + vone_ref[...])


def _premix(x, mod_l, lw, tabs, meta):
    t = x.shape[0]
    nt = t // TM
    row = lambda i, s, p: (i, 0)
    full = lambda i, s, p: (0, 0)
    bf = jnp.bfloat16
    outs = [jax.ShapeDtypeStruct((t, NA_WIDTH), bf)] * 3 + [jax.ShapeDtypeStruct((t, MLA_PAD_W), bf)] * 3 \
        + [jax.ShapeDtypeStruct((t, SEC_GATE), bf)]
    out_specs = [pl.BlockSpec((TM, NA_WIDTH), row)] * 3 + [pl.BlockSpec((TM, MLA_PAD_W), row)] * 3 \
        + [pl.BlockSpec((TM, SEC_GATE), row)]
    grid_spec = pltpu.PrefetchScalarGridSpec(
        num_scalar_prefetch=2,
        grid=(nt,),
        in_specs=[
            pl.BlockSpec((TM, D_MODEL), row),
            pl.BlockSpec((SEQ_PAD, 6 * D_MODEL), full),
            pl.BlockSpec((1, D_MODEL), full),
            pl.BlockSpec((D_MODEL, W_IN_P), full),
            pl.BlockSpec((1, MLA_Q_RANK), full),
            pl.BlockSpec((MLA_Q_RANK, 2 * MLA_PAD_W), full),
            pl.BlockSpec((1, MLA_KV_RANK), full),
            pl.BlockSpec((MLA_KV_RANK, 2 * MLA_PAD_W), full),
            pl.BlockSpec((TM, HEAD_PAD), lambda i, s, p: (p[i], 0)),
            pl.BlockSpec((TM, HEAD_PAD), lambda i, s, p: (p[i], 0)),
            pl.BlockSpec((1, MLA_PAD_W), full),
        ],
        out_specs=out_specs,
    )
    return pl.pallas_call(
        _premix_kernel, grid_spec=grid_spec, out_shape=outs,
        compiler_params=_cparams(("arbitrary",)), name="premix",
    )(meta["tile_seq"], meta["tile_pos"], x, mod_l, lw["g_mix"], lw["w_in"], lw["g_q"], lw["w_uq"],
      lw["g_kv"], lw["w_ukv"], tabs["rope_c"], tabs["rope_s"], tabs["v_ones"])


def _na_kernel(row0_ref, nrow_ref, q_ref, kp_ref, kc_ref, kn_ref, vp_ref, vc_ref, vn_ref, bias_ref, o_ref):
    j = pl.program_id(0)
    band_tok = NA_BAND * GRID_W
    half_tok = band_tok // 2
    ku = jnp.concatenate([kp_ref[half_tok:band_tok, :], kc_ref[...], kn_ref[0:half_tok, :]], axis=0)
    vu = jnp.concatenate([vp_ref[half_tok:band_tok, :], vc_ref[...], vn_ref[0:half_tok, :]], axis=0)
    r0_loc = j * NA_BAND - row0_ref[j]
    seq_rows = nrow_ref[j]
    lane = lax.broadcasted_iota(jnp.int32, (1, LANES), 1)
    low_half = lane < NA_HEAD_DIM

    entry = []
    for i in range(NA_BAND):
        rs = jnp.clip(r0_loc + i - NA_KH // 2, 0, seq_rows - NA_KH)
        row = []
        for u in range(NA_UNION):
            dri = u - i + (NA_KH - 1) - NA_BAND // 2
            if 0 <= dri < NA_MASKED:
                key_loc = r0_loc - NA_BAND // 2 + u
                row.append(jnp.where((key_loc >= rs) & (key_loc < rs + NA_KH), dri, NA_MASKED))
            else:
                row.append(None)
        entry.append(row)

    def scores(hd):
        sl = slice((hd // 2) * LANES, (hd // 2 + 1) * LANES)
        k2 = ku[:, sl]
        keep = low_half if hd % 2 == 0 else jnp.logical_not(low_half)
        return _dot_t(q_ref[:, sl], jnp.where(keep, k2, jnp.zeros_like(k2)))

    def biased(s, hd):
        base = hd * 2 * NA_ENTRIES
        rows = []
        for i in range(NA_BAND):
            blocks = []
            for xp in range(NA_UNION // 2):
                e0, e1 = entry[i][2 * xp], entry[i][2 * xp + 1]
                if e0 is None and e1 is None:
                    blocks.append(jnp.full((GRID_W, LANES), NEG_BIG, jnp.float32))
                    continue
                e0 = NA_MASKED if e0 is None else e0
                e1 = NA_MASKED if e1 is None else e1
                blk = s[i * GRID_W:(i + 1) * GRID_W, xp * LANES:(xp + 1) * LANES]
                blocks.append(blk + bias_ref[base + e0] + bias_ref[base + NA_ENTRIES + e1])
            rows.append(jnp.concatenate(blocks, axis=1))
        return jnp.concatenate(rows, axis=0)

    outs = []
    o_pair = None
    s_next = scores(0)
    for hd in range(NA_HEADS):
        s = s_next
        if hd + 1 < NA_HEADS:
            s_next = scores(hd + 1)
        s = biased(s, hd)
        m = jnp.max(s, axis=-1, keepdims=True)
        p = jnp.exp(s - m)
        l = jnp.sum(p, axis=-1, keepdims=True)
        sl = slice((hd // 2) * LANES, (hd // 2 + 1) * LANES)
        o = _dot(_bf(p), vu[:, sl]) / l
        if hd % 2 == 0:
            o_pair = o
        else:
            outs.append(jnp.where(low_half, o_pair, o))
    o_ref[...] = _bf(jnp.concatenate(outs, axis=-1))


def _na_attention(q, k, v, bias_tab, meta):
    t = q.shape[0]
    band_tok = NA_BAND * GRID_W
    nb = t // band_tok
    cur = lambda j, r0, nr, pb, nx: (j, 0)
    prv = lambda j, r0, nr, pb, nx: (pb[j], 0)
    nxt = lambda j, r0, nr, pb, nx: (nx[j], 0)
    blk = lambda f: pl.BlockSpec((band_tok, NA_WIDTH), f)

    def kern(r0, nr, pb, nx, *refs):
        _na_kernel(r0, nr, *refs)

    grid_spec = pltpu.PrefetchScalarGridSpec(
        num_scalar_prefetch=4,
        grid=(nb,),
        in_specs=[blk(cur), blk(prv), blk(cur), blk(nxt), blk(prv), blk(cur), blk(nxt),
                  pl.BlockSpec((NA_HEADS * 2 * NA_ENTRIES, GRID_W, LANES), lambda j, r0, nr, pb, nx: (0, 0, 0))],
        out_specs=blk(cur),
    )
    return pl.pallas_call(
        kern, grid_spec=grid_spec, out_shape=jax.ShapeDtypeStruct((t, NA_WIDTH), jnp.bfloat16),
        compiler_params=_cparams(("arbitrary",)), name="na_attention",
    )(meta["band_row0"], meta["band_nrow"], meta["band_prev"], meta["band_next"], q, k, k, k, v, v, v, bias_tab)


def _mla_kernel(qi_ref, kj_ref, first_ref, last_ref, q_ref, k_ref, v_ref, o_ref, m_scr, acc_scr):
    n = pl.program_id(0)

    @pl.when(first_ref[n] == 1)
    def _():
        m_scr[...] = jnp.full(m_scr.shape, NEG_BIG, jnp.float32)
        acc_scr[...] = jnp.zeros(acc_scr.shape, jnp.float32)

    def scores(hd):
        sl = slice(hd * HEAD_PAD, (hd + 1) * HEAD_PAD)
        return _dot_t(q_ref[:, sl], k_ref[:, sl])

    s_next = scores(0)
    for hd in range(MLA_HEADS):
        sl = slice(hd * HEAD_PAD, (hd + 1) * HEAD_PAD)
        s = s_next
        if hd + 1 < MLA_HEADS:
            s_next = scores(hd + 1)
        m_prev = m_scr[hd]
        m_new = jnp.maximum(m_prev, jnp.max(s, axis=-1, keepdims=True))
        p = jnp.exp2(s - m_new[:, 0:1])
        alpha = jnp.exp2(m_prev - m_new)
        acc_scr[hd] = alpha * acc_scr[hd] + _dot(_bf(p), v_ref[:, sl])
        m_scr[hd] = m_new

    @pl.when(last_ref[n] == 1)
    def _():
        for hd in range(MLA_HEADS):
            a = acc_scr[hd]
            o_ref[:, hd * HEAD_PAD:(hd + 1) * HEAD_PAD] = _bf(a / a[:, MLA_V:MLA_V + 1])


def _mla_attention(qm, km, vm, meta):
    t = qm.shape[0]
    n_pairs = meta["mla_qi"].shape[0]
    grid_spec = pltpu.PrefetchScalarGridSpec(
        num_scalar_prefetch=4,
        grid=(n_pairs,),
        in_specs=[
            pl.BlockSpec((TQ, MLA_PAD_W), lambda n, qi, kj, f, l: (qi[n], 0)),
            pl.BlockSpec((TK, MLA_PAD_W), lambda n, qi, kj, f, l: (kj[n], 0)),
            pl.BlockSpec((TK, MLA_PAD_W), lambda n, qi, kj, f, l: (kj[n], 0)),
        ],
        out_specs=pl.BlockSpec((TQ, MLA_PAD_W), lambda n, qi, kj, f, l: (qi[n], 0)),
        scratch_shapes=[pltpu.VMEM((MLA_HEADS, TQ, HEAD_PAD), jnp.float32),
                        pltpu.VMEM((MLA_HEADS, TQ, HEAD_PAD), jnp.float32)],
    )
    return pl.pallas_call(
        _mla_kernel, grid_spec=grid_spec, out_shape=jax.ShapeDtypeStruct((t, MLA_PAD_W), jnp.bfloat16),
        compiler_params=_cparams(("arbitrary",)), name="mla_attention",
    )(meta["mla_qi"], meta["mla_kj"], meta["mla_first"], meta["mla_last"], qm, km, vm)


def _postmix_kernel(seq_ref, x_ref, yna_ref, ymla_ref, gate_ref, mod_ref, wna_ref, wmla_ref, wout_ref,
                    g_ref, wr_hi_ref, wr_lo_ref, br_ref,
                    xo_ref, h2_ref, eid_ref, wt_ref):
    i = pl.program_id(0)
    sid = seq_ref[i]
    d = D_MODEL
    g1 = mod_ref[pl.ds(sid, 1), 2 * d:3 * d]
    sh2 = mod_ref[pl.ds(sid, 1), 3 * d:4 * d]
    sc2 = mod_ref[pl.ds(sid, 1), 4 * d:5 * d]
    a = _dot(yna_ref[...], wna_ref[...])
    b = _dot(ymla_ref[...], wmla_ref[...])
    gate = gate_ref[...]
    merged = gate[:, 0:d].astype(jnp.float32) * a + gate[:, d:2 * d].astype(jnp.float32) * b
    xn = x_ref[...] + g1 * _dot(_bf(merged), wout_ref[...])
    xo_ref[...] = xn
    h2 = (_rms(xn) * g_ref[...]) * (1.0 + sc2) + sh2
    h2_ref[...] = h2

    h_hi = _bf(h2)
    h_lo = _bf(h2 - h_hi.astype(jnp.float32))
    logits = (_dot(h_hi, wr_hi_ref[...]) + _dot(h_hi, wr_lo_ref[...]) + _dot(h_lo, wr_hi_ref[...])) + br_ref[...]
    lane = lax.broadcasted_iota(jnp.int32, logits.shape, 1)
    is_g = lane < N_GROUPS
    gl = jnp.where(is_g, logits, NEG_BIG)
    gmax = jnp.max(gl, axis=-1, keepdims=True)
    gsum = jnp.sum(jnp.where(is_g, jnp.exp(gl - gmax), 0.0), axis=-1, keepdims=True)
    g_top = 1.0 / gsum
    g_idx = jnp.min(jnp.where(gl == gmax, lane, LANES), axis=-1, keepdims=True)
    e_lo = N_GROUPS + g_idx * EXPERTS_PER_GROUP
    in_grp = (lane >= e_lo) & (lane < e_lo + EXPERTS_PER_GROUP)
    el = jnp.where(in_grp, logits, NEG_BIG)
    m1 = jnp.max(el, axis=-1, keepdims=True)
    ep = jnp.where(in_grp, jnp.exp(el - m1), 0.0)
    ep = ep / jnp.sum(ep, axis=-1, keepdims=True)
    p1 = jnp.max(ep, axis=-1, keepdims=True)
    i1 = jnp.min(jnp.where(in_grp & (ep == p1), lane, LANES), axis=-1, keepdims=True)
    rest = in_grp & (lane != i1)
    p2 = jnp.max(jnp.where(rest, ep, -1.0), axis=-1, keepdims=True)
    i2 = jnp.min(jnp.where(rest & (ep == p2), lane, LANES), axis=-1, keepdims=True)
    denom = p1 + p2
    w1 = g_top * p1 / denom
    w2 = g_top * p2 / denom
    eid_ref[...] = jnp.where(lane == 0, i1 - N_GROUPS, jnp.where(lane == 1, i2 - N_GROUPS, 0))
    wt_ref[...] = jnp.where(lane == 0, w1, jnp.where(lane == 1, w2, 0.0))


def _postmix(x, yna, ymla, gates, mod_l, lw, meta):
    t = x.shape[0]
    nt = t // TM
    row = lambda i, s: (i, 0)
    full = lambda i, s: (0, 0)
    d = D_MODEL
    grid_spec = pltpu.PrefetchScalarGridSpec(
        num_scalar_prefetch=1,
        grid=(nt,),
        in_specs=[
            pl.BlockSpec((TM, d), row),
            pl.BlockSpec((TM, NA_WIDTH), row),
            pl.BlockSpec((TM, MLA_PAD_W), row),
            pl.BlockSpec((TM, SEC_GATE), row),
            pl.BlockSpec((SEQ_PAD, 6 * d), full),
            pl.BlockSpec((NA_WIDTH, d), full),
            pl.BlockSpec((MLA_PAD_W, d), full),
            pl.BlockSpec((d, d), full),
            pl.BlockSpec((1, d), full),
            pl.BlockSpec((d, LANES), full),
            pl.BlockSpec((d, LANES), full),
            pl.BlockSpec((1, LANES), full),
        ],
        out_specs=[pl.BlockSpec((TM, d), row), pl.BlockSpec((TM, d), row),
                   pl.BlockSpec((TM, LANES), row), pl.BlockSpec((TM, LANES), row)],
    )
    return pl.pallas_call(
        _postmix_kernel, grid_spec=grid_spec,
        out_shape=[jax.ShapeDtypeStruct((t, d), jnp.float32), jax.ShapeDtypeStruct((t, d), jnp.float32),
                   jax.ShapeDtypeStruct((t, LANES), jnp.int32), jax.ShapeDtypeStruct((t, LANES), jnp.float32)],
        compiler_params=_cparams(("arbitrary",)), name="postmix",
    )(meta["tile_seq"], x, yna, ymla, gates, mod_l, lw["w_na_o"], lw["w_mla_o"], lw["w_out"], lw["g_ffn"],
      lw["w_r_hi"], lw["w_r_lo"], lw["b_r"])


def _expert_kernel(be_ref, nu_ref, tok_ref, row_ref, half_ref, h_hbm, w1_ref, w3_ref, w2_ref, y_hbm,
                   xbuf, ybuf, sem_in, sem_out):
    j = pl.program_id(0)
    n_used = nu_ref[0]
    d = D_MODEL

    def wait_gather(slot):
        pltpu.make_async_copy(h_hbm.at[pl.ds(0, BM)], xbuf.at[slot], sem_in.at[slot]).wait()

    def wait_scatter(slot):
        pltpu.make_async_copy(ybuf.at[slot], y_hbm.at[pl.ds(0, BM), pl.ds(0, d)], sem_out.at[slot]).wait()

    @pl.when(j == 0)
    def _():
        ybuf[1] = jnp.zeros((BM, d), jnp.float32)
        spare0 = y_hbm.shape[0] - 2 * BM
        fills = [pltpu.make_async_copy(ybuf.at[1], y_hbm.at[pl.ds(spare0 + s * BM, BM), pl.ds(half * d, d)],
                                       sem_out.at[1]) for s in range(2) for half in range(2)]
        for cp in fills:
            cp.start()
        for cp in fills:
            cp.wait()

    @pl.when(j < n_used)
    def _():
        slot = lax.rem(j, 2)
        for r in range(BM):
            pltpu.make_async_copy(h_hbm.at[pl.ds(tok_ref[0, 0, r], 1)], xbuf.at[slot, pl.ds(r, 1)],
                                  sem_in.at[slot]).start()

    @pl.when((j >= 1) & (j <= n_used))
    def _():
        i = j - 1
        slot = lax.rem(i, 2)
        wait_gather(slot)

        @pl.when(i >= 2)
        def _():
            wait_scatter(slot)

        xb = _bf(xbuf[slot])
        a = _dot(xb, w1_ref[0])
        b = _dot(xb, w3_ref[0])
        hmid = _bf((a * jax.nn.sigmoid(a)) * b)
        ybuf[slot] = _dot(hmid, w2_ref[0])
        for r in range(BM):
            col = pl.multiple_of(half_ref[0, 0, r] * d, d)
            pltpu.make_async_copy(ybuf.at[slot, pl.ds(r, 1)],
                                  y_hbm.at[pl.ds(row_ref[0, 0, r], 1), pl.ds(col, d)],
                                  sem_out.at[slot]).start()

        @pl.when(j == n_used)
        def _():
            @pl.when(i >= 1)
            def _():
                wait_scatter(1 - slot)

            wait_scatter(slot)


def _experts(h2, lw, disp):
    d = D_MODEL
    t = h2.shape[0]
    n_blocks = disp["block_expert"].shape[0]
    last = n_blocks - 1
    prev_blk = lambda j, be, nu: (jnp.maximum(j - 1, 0), 0, 0)
    prev_w = lambda j, be, nu: (be[jnp.maximum(j - 1, 0)], 0, 0)
    grid_spec = pltpu.PrefetchScalarGridSpec(
        num_scalar_prefetch=2,
        grid=(n_blocks + 1,),
        in_specs=[
            pl.BlockSpec((1, 1, BM), lambda j, be, nu: (jnp.minimum(j, last), 0, 0), memory_space=pltpu.SMEM),
            pl.BlockSpec((1, 1, BM), prev_blk, memory_space=pltpu.SMEM),
            pl.BlockSpec((1, 1, BM), prev_blk, memory_space=pltpu.SMEM),
            pl.BlockSpec(memory_space=pl.ANY),
            pl.BlockSpec((1, d, D_EXPERT), prev_w),
            pl.BlockSpec((1, d, D_EXPERT), prev_w),
            pl.BlockSpec((1, D_EXPERT, d), prev_w),
        ],
        out_specs=pl.BlockSpec(memory_space=pl.ANY),
        scratch_shapes=[pltpu.VMEM((2, BM, d), jnp.float32), pltpu.VMEM((2, BM, d), jnp.float32),
                        pltpu.SemaphoreType.DMA((2,)), pltpu.SemaphoreType.DMA((2,))],
    )
    return pl.pallas_call(
        _expert_kernel, grid_spec=grid_spec, out_shape=jax.ShapeDtypeStruct((t + 2 * BM, 2 * d), jnp.float32),
        compiler_params=_cparams(("arbitrary",)), name="experts",
    )(disp["block_expert"], disp["n_used"], disp["src_tok"], disp["dst_row"], disp["dst_half"], h2,
      lw["w1"], lw["w3"], lw["w2"])


def _dispatch(eid):
    t = eid.shape[0]
    a = 2 * t
    e_flat = eid.reshape(a)
    order = jnp.argsort(e_flat, stable=True).astype(jnp.int32)
    counts = jnp.sum((e_flat[:, None] == jnp.arange(N_EXPERTS, dtype=jnp.int32)[None, :]).astype(jnp.int32), axis=0)
    starts = jnp.cumsum(counts) - counts
    padded = (counts + BM - 1) // BM * BM
    pad_ends = jnp.cumsum(padded)
    pad_starts = pad_ends - padded
    n_blocks = (a + N_EXPERTS * (BM - 1) + BM - 1) // BM
    blk0 = jnp.arange(n_blocks, dtype=jnp.int32) * BM
    block_expert = jnp.minimum(jnp.searchsorted(pad_ends, blk0, side="right"), N_EXPERTS - 1).astype(jnp.int32)
    in_expert = blk0 - pad_starts[block_expert]
    block_nvalid = jnp.clip(counts[block_expert] - in_expert, 0, BM).astype(jnp.int32)
    lane = jnp.arange(BM, dtype=jnp.int32)[None, :]
    valid = lane < block_nvalid[:, None]
    sorted_pos = jnp.clip(starts[block_expert][:, None] + in_expert[:, None] + lane, 0, a - 1)
    slot = order[sorted_pos]
    return {
        "block_expert": block_expert,
        "n_used": (pad_ends[-1:] // BM).astype(jnp.int32),
        "src_tok": jnp.where(valid, slot // 2, 0).reshape(n_blocks, 1, BM),
        "dst_row": jnp.where(valid, slot // 2, t + (blk0[:, None] // BM % 2) * BM + lane).reshape(n_blocks, 1, BM),
        "dst_half": jnp.where(valid, slot % 2, 0).reshape(n_blocks, 1, BM),
    }


def _combine_kernel(seq_ref, x_ref, y_ref, wt_ref, mod_ref, gf_ref, o_ref, *, final):
    i = pl.program_id(0)
    sid = seq_ref[i]
    d = D_MODEL
    g2 = mod_ref[pl.ds(sid, 1), 5 * d:6 * d]
    wt = wt_ref[...]
    moe = y_ref[:, 0:d] * wt[:, 0:1] + y_ref[:, d:2 * d] * wt[:, 1:2]
    xn = x_ref[...] + g2 * moe
    if final:
        xn = _rms(xn) * gf_ref[...]
    o_ref[...] = xn


def _combine(x, y_tok, wts, mod_l, g_final, meta, final):
    t = x.shape[0]
    nt = t // TM
    d = D_MODEL
    row = lambda i, s: (i, 0)
    full = lambda i, s: (0, 0)
    grid_spec = pltpu.PrefetchScalarGridSpec(
        num_scalar_prefetch=1,
        grid=(nt,),
        in_specs=[pl.BlockSpec((TM, d), row), pl.BlockSpec((TM, 2 * d), row), pl.BlockSpec((TM, LANES), row),
                  pl.BlockSpec((SEQ_PAD, 6 * d), full), pl.BlockSpec((1, d), full)],
        out_specs=pl.BlockSpec((TM, d), row),
    )
    return pl.pallas_call(
        functools.partial(_combine_kernel, final=final), grid_spec=grid_spec,
        out_shape=jax.ShapeDtypeStruct((t, d), jnp.float32),
        compiler_params=_cparams(("arbitrary",)), name="combine",
    )(meta["tile_seq"], x, y_tok, wts, mod_l, g_final)


def _static_meta(seqs):
    tile_seq, tile_pos = [], []
    band_row0, band_nrow, band_prev, band_next = [], [], [], []
    qi, kj, first, last = [], [], [], []
    sid, tok0 = 0, 0
    band_tok = NA_BAND * GRID_W
    for b, s in seqs:
        assert s % TK == 0 and s % TQ == 0 and s % TM == 0 and s % band_tok == 0 and s // GRID_W >= NA_KH
        for _ in range(b):
            for i in range(s // TM):
                tile_seq.append(sid)
                tile_pos.append(i)
            nb, b0 = s // band_tok, tok0 // band_tok
            for j in range(nb):
                band_row0.append(tok0 // GRID_W)
                band_nrow.append(s // GRID_W)
                band_prev.append(b0 + max(j - 1, 0))
                band_next.append(b0 + min(j + 1, nb - 1))
            for i in range(s // TQ):
                for j in range(s // TK):
                    qi.append(tok0 // TQ + i)
                    kj.append(tok0 // TK + j)
                    first.append(int(j == 0))
                    last.append(int(j == s // TK - 1))
            sid += 1
            tok0 += s
    arr = lambda v: jnp.asarray(np.asarray(v, np.int32))
    return {"tile_seq": arr(tile_seq), "tile_pos": arr(tile_pos), "band_row0": arr(band_row0),
            "band_nrow": arr(band_nrow), "band_prev": arr(band_prev), "band_next": arr(band_next),
            "mla_qi": arr(qi), "mla_kj": arr(kj), "mla_first": arr(first), "mla_last": arr(last)}


def _tables(max_seq):
    inv = 1.0 / (ROPE_THETA ** (jnp.arange(0, MLA_ROPE, 2, dtype=jnp.float32) / MLA_ROPE))
    ang = jnp.arange(max_seq, dtype=jnp.float32)[:, None] * inv[None, :]
    cos, sin = jnp.cos(ang), jnp.sin(ang)
    ones = jnp.ones((max_seq, MLA_NOPE), jnp.float32)
    zpad = jnp.zeros((max_seq, HEAD_PAD - MLA_NOPE - MLA_ROPE), jnp.float32)
    rope_c = jnp.concatenate([ones, cos, cos, zpad], axis=1)
    rope_s = jnp.concatenate([0.0 * ones, sin, sin, zpad], axis=1)
    v_ones = jnp.zeros((MLA_HEADS, HEAD_PAD), jnp.float32).at[:, MLA_V].set(1.0).reshape(1, MLA_PAD_W)
    return {"rope_c": rope_c, "rope_s": rope_s, "v_ones": v_ones}


def _pad_heads(w, width):
    k = w.shape[0]
    w = w.reshape(k, MLA_HEADS, width)
    return jnp.pad(w, ((0, 0), (0, 0), (0, HEAD_PAD - width))).reshape(k, MLA_PAD_W)


def _rope_swap(w_rope):
    half = MLA_ROPE // 2
    return jnp.concatenate([-w_rope[..., half:], w_rope[..., :half]], axis=-1)


def _na_bias_table(rpb):
    col = jnp.arange(GRID_W)
    col_start = jnp.clip(col - NA_KW // 2, 0, GRID_W - NA_KW)
    kc = jnp.arange(GRID_W)
    inside = (kc[None, :] >= col_start[:, None]) & (kc[None, :] < col_start[:, None] + NA_KW)
    dc = jnp.clip(kc[None, :] - col[:, None] + NA_KW - 1, 0, 2 * NA_KW - 2)
    blocks = jnp.where(inside[None, None], rpb[:, :, dc], NEG_BIG)
    masked = jnp.full((NA_HEADS, 1, GRID_W, GRID_W), NEG_BIG, jnp.float32)
    blocks = jnp.concatenate([blocks, masked], axis=1)
    zeros = jnp.zeros_like(blocks)
    sides = jnp.stack([jnp.concatenate([blocks, zeros], axis=-1), jnp.concatenate([zeros, blocks], axis=-1)], axis=1)
    return sides.reshape(NA_HEADS * 2 * NA_ENTRIES, GRID_W, LANES)


def _layer_weights(l, p):
    bf = jnp.bfloat16
    w_in = p["w_in"][l]
    o = 3 * NA_WIDTH
    w_q, w_k, w_v = w_in[:, 0:NA_WIDTH], w_in[:, NA_WIDTH:2 * NA_WIDTH], w_in[:, 2 * NA_WIDTH:o]
    w_cq = w_in[:, o:o + MLA_Q_RANK]
    w_ckv = w_in[:, o + MLA_Q_RANK:o + MLA_Q_RANK + MLA_KV_RANK]
    o2 = o + MLA_Q_RANK + MLA_KV_RANK
    w_kr = w_in[:, o2:o2 + MLA_ROPE]
    w_gate = w_in[:, o2 + MLA_ROPE:]
    zl = jnp.zeros((D_MODEL, MLA_NOPE), jnp.float32)
    zr = jnp.zeros((D_MODEL, HEAD_PAD - MLA_NOPE - MLA_ROPE), jnp.float32)
    kr_a = jnp.concatenate([zl, w_kr, zr], axis=1)
    kr_b = jnp.concatenate([zl, _rope_swap(w_kr), zr], axis=1)
    w_in_p = jnp.concatenate([w_q * (NA_HEAD_DIM ** -0.5), w_k, w_v, w_cq, w_ckv, kr_a, kr_b, w_gate], axis=1)

    w_uq = p["w_uq"][l].reshape(MLA_Q_RANK, MLA_HEADS, MLA_NOPE + MLA_ROPE)
    uq_plain = _pad_heads(w_uq.reshape(MLA_Q_RANK, -1), MLA_NOPE + MLA_ROPE)
    uq_swap = jnp.concatenate([jnp.zeros_like(w_uq[..., :MLA_NOPE]), _rope_swap(w_uq[..., MLA_NOPE:])], axis=-1)
    uq_swap = _pad_heads(uq_swap.reshape(MLA_Q_RANK, -1), MLA_NOPE + MLA_ROPE)
    w_ukv = p["w_ukv"][l].reshape(MLA_KV_RANK, MLA_HEADS, MLA_NOPE + MLA_V)
    uk = _pad_heads(w_ukv[..., :MLA_NOPE].reshape(MLA_KV_RANK, -1), MLA_NOPE)
    uv = _pad_heads(w_ukv[..., MLA_NOPE:].reshape(MLA_KV_RANK, -1), MLA_V)
    w_mla_o = p["w_mla_o"][l].reshape(MLA_HEADS, MLA_V, D_MODEL)
    w_mla_o = jnp.pad(w_mla_o, ((0, 0), (0, HEAD_PAD - MLA_V), (0, 0))).reshape(MLA_PAD_W, D_MODEL)

    w_r = jnp.concatenate([p["router_wg"][l], p["router_we"][l]], axis=1)
    w_r = jnp.pad(w_r, ((0, 0), (0, LANES - w_r.shape[1])))
    w_r_hi = w_r.astype(bf)
    b_r = jnp.concatenate([p["router_bg"][l], p["router_be"][l]])
    b_r = jnp.pad(b_r, (0, LANES - b_r.shape[0])).reshape(1, LANES)
    return {
        "g_mix": p["norm_mix_g"][l].reshape(1, D_MODEL),
        "w_in": w_in_p.astype(bf),
        "g_q": p["mla_q_norm_g"][l].reshape(1, MLA_Q_RANK),
        "w_uq": jnp.concatenate([uq_plain, uq_swap], axis=1).astype(bf),
        "g_kv": p["mla_kv_norm_g"][l].reshape(1, MLA_KV_RANK),
        "w_ukv": jnp.concatenate([uk, uv], axis=1).astype(bf),
        "na_bias": _na_bias_table(p["na_rpb"][l]),
        "w_na_o": p["w_na_o"][l].astype(bf),
        "w_mla_o": w_mla_o.astype(bf),
        "w_out": p["w_out"][l].astype(bf),
        "g_ffn": p["norm_ffn_g"][l].reshape(1, D_MODEL),
        "w_r_hi": w_r_hi,
        "w_r_lo": (w_r - w_r_hi.astype(jnp.float32)).astype(bf),
        "b_r": b_r,
        "w1": p["expert_w1"][l].astype(bf),
        "w3": p["expert_w3"][l].astype(bf),
        "w2": p["expert_w2"][l].astype(bf),
    }


def _forward(x_flat, c_all, seqs, p):
    t = x_flat.shape[0]
    n_seq = c_all.shape[0]
    assert n_seq <= SEQ_PAD
    meta = _static_meta(seqs)
    tabs = _tables(max(s for _, s in seqs))
    c_pad = jnp.pad(c_all, ((0, SEQ_PAD - n_seq), (0, 0)))
    mod = _modulation(c_pad, p["ada_w"], p["ada_b"])
    g_final = p["final_norm_g"].reshape(1, D_MODEL)
    x = x_flat
    for l in range(DEPTH):
        lw = _layer_weights(l, p)
        qna, kna, vna, qm, km, vm, gates = _premix(x, mod[l], lw, tabs, meta)
        yna = _na_attention(qna, kna, vna, lw["na_bias"], meta)
        ymla = _mla_attention(qm, km, vm, meta)
        x, h2, eid, wts = _postmix(x, yna, ymla, gates, mod[l], lw, meta)
        disp = _dispatch(eid[:, 0:2])
        y_tok = _experts(h2, lw, disp)
        x = _combine(x, y_tok, wts, mod[l], g_final, meta, final=(l == DEPTH - 1))
    return x


def kernel(x_prompt, x_sample, c_prompt, c_sample, ada_w, ada_b, norm_mix_g, w_in, mla_q_norm_g, w_uq,
           mla_kv_norm_g, w_ukv, na_rpb, w_na_o, w_mla_o, w_out, norm_ffn_g, router_wg, router_bg, router_we,
           router_be, expert_w1, expert_w3, expert_w2, final_norm_g):
    p = dict(ada_w=ada_w, ada_b=ada_b, norm_mix_g=norm_mix_g, w_in=w_in, mla_q_norm_g=mla_q_norm_g, w_uq=w_uq,
             mla_kv_norm_g=mla_kv_norm_g, w_ukv=w_ukv, na_rpb=na_rpb, w_na_o=w_na_o, w_mla_o=w_mla_o, w_out=w_out,
             norm_ffn_g=norm_ffn_g, router_wg=router_wg, router_bg=router_bg, router_we=router_we,
             router_be=router_be, expert_w1=expert_w1, expert_w3=expert_w3, expert_w2=expert_w2,
             final_norm_g=final_norm_g)
    bp, sp, d = x_prompt.shape
    bs, ss, _ = x_sample.shape
    x_flat = jnp.concatenate([x_prompt.reshape(bp * sp, d), x_sample.reshape(bs * ss, d)], axis=0)
    c_all = jnp.concatenate([c_prompt, c_sample], axis=0)
    y = _forward(x_flat, c_all, [(bp, sp), (bs, ss)], p)
    return y[:bp * sp].reshape(bp, sp, d), y[bp * sp:].reshape(bs, ss, d)
```
